```python
import math
import jax
import jax.numpy as jnp
from jax import lax
import numpy as np

D_MODEL = 1024
BATCH = 4
SEQ = 8192
DEPTH = 2
DEC_BATCH = 32
DEC_SEQ = 4
PAST_LEN = 16384
PAGE_SIZE = 128

N_EVEN = (DEPTH + 1) // 2
N_ODD = DEPTH // 2
A_WINDOWS = (128, 512, 2048)
A_DILATIONS = (1, 4, 16)
A_GROUPS = 3
A_HEADS = 8
HEAD_DIM = 64
A_WIDTH = A_HEADS * HEAD_DIM
A_SPAN = A_WINDOWS[0] // A_DILATIONS[0]
A_BLOCK = 128
ATTN_SCALE = HEAD_DIM ** -0.5
REL_BUCKETS = 32
REL_MAX_DIST = 2048
B_WIDTH = D_MODEL // 2
B_GROUP_CH = 16
B_GROUPS = B_WIDTH // B_GROUP_CH
B_STATE = 64
DT_MIN = 0.001
DT_MAX = 0.1
C_HEADS = 4
C_DK = D_MODEL // 2 // C_HEADS
C_DV = D_MODEL // C_HEADS
C_GATE_RANK = 16
C_GATE_TAU = 16.0
C_CHUNK = 64
D_FF = D_MODEL * 11008 // 4096
N_EXPERTS = 8
TOP_K = 2
D_EXPERT = D_MODEL * 14336 // 4096
IN_EVEN = A_GROUPS * 3 * A_WIDTH + B_WIDTH
IN_ODD = 2 * C_HEADS * C_DK + 2 * C_HEADS * C_DV + C_GATE_RANK
RMS_EPS = 1e-6
NEG_INF = -1e30

kernel_name = 'hybrid_dilated_s5_gla_moe_step'


def rms_norm(x, g):
    xf = x.astype(jnp.float32)
    y = xf * lax.rsqrt(jnp.mean(xf * xf, axis=-1, keepdims=True) + RMS_EPS)
    return (y * g.astype(jnp.float32)).astype(x.dtype)


def rel_bucket(dist):
    n = np.maximum(np.asarray(dist), 0)
    exact = REL_BUCKETS // 2
    scaled = np.log(np.maximum(n, 1) / exact) / np.log(REL_MAX_DIST / exact)
    large = np.minimum(exact + (scaled * (REL_BUCKETS - exact)).astype(np.int32), REL_BUCKETS - 1)
    return np.where(n < exact, n, large).astype(np.int32)


def dilated_attn_prompt(q, k, v, bias_tab, r):
    bsz, seq, nh, dh = q.shape
    sub_len = seq // r
    nb = -(-sub_len // A_BLOCK)
    lp = nb * A_BLOCK

    def sub(t):
        t = t.reshape(bsz, sub_len, r, nh, dh)
        return jnp.pad(t, ((0, 0), (0, lp - sub_len), (0, 0), (0, 0), (0, 0)))

    def key_blocks(t):
        t = jnp.pad(sub(t), ((0, 0), (A_BLOCK, 0), (0, 0), (0, 0), (0, 0)))
        t = t.reshape(bsz, nb + 1, A_BLOCK, r, nh, dh)
        return jnp.concatenate([t[:, :-1], t[:, 1:]], axis=2)

    qb = sub(q).reshape(bsz, nb, A_BLOCK, r, nh, dh)
    kb, vb = key_blocks(k), key_blocks(v)
    qi = np.arange(A_BLOCK)[:, None]
    kj = np.arange(2 * A_BLOCK)[None, :]
    rel = qi + A_BLOCK - kj
    band = (rel >= 0) & (rel <= A_SPAN)
    key_ok = (np.arange(nb)[:, None, None] * A_BLOCK + kj[None] - A_BLOCK) >= 0
    mask = (band[None] & key_ok)[None, :, None, None]
    bias = jnp.transpose(bias_tab[rel_bucket(r * rel)], (2, 0, 1)).astype(jnp.float32)
    s = jnp.einsum('bnqchd,bnkchd->bnchqk', qb, kb, preferred_element_type=jnp.float32) * ATTN_SCALE + bias
    s = jnp.where(mask, s, NEG_INF)
    lse = jax.nn.logsumexp(s, axis=-1)
    p = jnp.exp(s - lse[..., None]).astype(v.dtype)
    o = jnp.einsum('bnchqk,bnkchd->bnqchd', p, vb)
    o = o.reshape(bsz, lp, r, nh, dh)[:, :sub_len].reshape(bsz, seq, nh, dh)
    lse = jnp.transpose(lse, (0, 1, 4, 2, 3)).reshape(bsz, lp, r, nh)[:, :sub_len].reshape(bsz, seq, nh)
    return o, lse


def dilated_attn_sample(q, k_all, v_all, bias_tab, r, n_past):
    t_new = q.shape[1]
    m = np.arange(A_SPAN + 1)
    idx = n_past + np.arange(t_new)[:, None] - r * m[None, :]
    valid = idx >= 0
    idx = np.maximum(idx, 0)
    kg = k_all[:, idx]
    vg = v_all[:, idx]
    bias = bias_tab[rel_bucket(r * m)].T.astype(jnp.float32)
    s = jnp.einsum('bthd,btmhd->bhtm', q, kg, preferred_element_type=jnp.float32) * ATTN_SCALE + bias[:, None, :]
    s = jnp.where(valid, s, NEG_INF)
    lse = jax.nn.logsumexp(s, axis=-1)
    p = jnp.exp(s - lse[..., None]).astype(v_all.dtype)
    o = jnp.einsum('bhtm,btmhd->bthd', p, vg)
    return o, jnp.transpose(lse, (0, 2, 1))


def s5_mixer(u, h0, lam_re, lam_im, log_dt, b_re, b_im, c_re, c_im, d_skip, w_glu, b_glu):
    f32 = jnp.float32
    bsz, t_len, _ = u.shape
    lam = lax.complex(lam_re.astype(f32), lam_im.astype(f32))
    lam_bar = jnp.exp(lam * jnp.exp(log_dt.astype(f32))[:, None])
    b_bar = ((lam_bar - 1.0) / lam)[..., None] * lax.complex(b_re.astype(f32), b_im.astype(f32))
    c_mat = lax.complex(c_re.astype(f32), c_im.astype(f32))
    uf = u.astype(f32)
    bu = jnp.einsum('btgc,gpc->tbgp', uf.reshape(bsz, t_len, B_GROUPS, B_GROUP_CH), b_bar)
    if h0 is not None:
        bu = bu.at[0].add(lam_bar * lax.complex(h0[..., 0].astype(f32), h0[..., 1].astype(f32)))
    a = jnp.broadcast_to(lam_bar, (t_len, 1, B_GROUPS, B_STATE))

    def combine(e1, e2):
        return e1[0] * e2[0], e2[0] * e1[1] + e2[1]

    _, h = lax.associative_scan(combine, (a, bu), axis=0)
    y = jnp.einsum('tbgp,gcp->btgc', h, c_mat).real.reshape(bsz, t_len, B_WIDTH)
    y = jax.nn.gelu(y + d_skip.astype(f32) * uf)
    y = y * jax.nn.sigmoid(y @ w_glu.astype(f32) + b_glu.astype(f32))
    h_last = h[-1]
    return y.astype(u.dtype), jnp.stack([h_last.real, h_last.imag], axis=-1)


def even_mixer(h, w_in, q_gain, k_gain, rel_bias, lam_re, lam_im, log_dt, b_re, b_im, c_re, c_im,
               d_skip, w_glu, b_glu, w_out, a_bufs, b_state):
    bsz, t_len, _ = h.shape
    proj = h @ w_in
    n_a = A_GROUPS * 3 * A_WIDTH
    pa = proj[..., :n_a].reshape(bsz, t_len, A_GROUPS, 3, A_HEADS, HEAD_DIM)
    outs, lses, new_kv = [], [], []
    for g in range(A_GROUPS):
        w, r = A_WINDOWS[g], A_DILATIONS[g]
        q = rms_norm(pa[:, :, g, 0], q_gain[g])
        k = rms_norm(pa[:, :, g, 1], k_gain[g])
        v = pa[:, :, g, 2]
        bias_g = rel_bias[:, g * A_HEADS:(g + 1) * A_HEADS]
        kv = jnp.stack([k, v], axis=2)
        if a_bufs is None:
            o, lse = dilated_attn_prompt(q, k, v, bias_g, r)
            new_kv.append(kv[:, t_len - min(w, t_len):])
        else:
            buf = a_bufs[g]
            o, lse = dilated_attn_sample(q, jnp.concatenate([buf[:, :, 0], k], axis=1),
                                         jnp.concatenate([buf[:, :, 1], v], axis=1), bias_g, r, buf.shape[1])
            new_kv.append(kv)
        outs.append(o)
        lses.append(lse)
    wts = jax.nn.softmax(jnp.stack(lses), axis=0)[..., None]
    o_a = jnp.sum(wts * jnp.stack(outs).astype(jnp.float32), axis=0).astype(h.dtype).reshape(bsz, t_len, A_WIDTH)
    o_b, new_b = s5_mixer(proj[..., n_a:], b_state, lam_re, lam_im, log_dt, b_re, b_im, c_re, c_im,
                          d_skip, w_glu, b_glu)
    return jnp.concatenate([o_a, o_b], axis=-1) @ w_out, new_kv, new_b


def gla_chunked(q, k, v, log_a, s0, chunk):
    f32 = jnp.float32
    bsz, t_len, nh, dk = q.shape
    dv = v.shape[-1]
    nc = t_len // chunk

    def cm(t):
        return t.astype(f32).reshape(bsz, nc, chunk, nh, t.shape[-1]).transpose(1, 0, 2, 3, 4)

    qc, kc, vc, ac = cm(q), cm(k), cm(v), cm(log_a)
    b = jnp.cumsum(ac, axis=2)
    b_last = b[:, :, -1:]
    q_dec = qc * jnp.exp(b)
    k_tail = kc * jnp.exp(b_last - b)
    att = jnp.einsum('nbthk,nbshk->nbhts', q_dec, kc * jnp.exp(-b))
    att = jnp.where(np.tril(np.ones((chunk, chunk), dtype=bool)), att, 0.0)
    o_intra = jnp.einsum('nbhts,nbshv->nbthv', att, vc)

    def step(s, xs):
        qd, kt, vv, dec = xs
        o = jnp.einsum('bthk,bhkv->bthv', qd, s)
        s = s * dec[..., None] + jnp.einsum('bshk,bshv->bhkv', kt, vv)
        return s, o

    s_init = jnp.zeros((bsz, nh, dk, dv), f32) if s0 is None else s0.astype(f32)
    s_last, o_inter = lax.scan(step, s_init, (q_dec, k_tail, vc, jnp.exp(b_last[:, :, 0])))
    o = (o_intra + o_inter).transpose(1, 0, 2, 3, 4).reshape(bsz, t_len, nh, dv)
    return o, s_last


def gla_mixer(h, w_in, w_gate_lr, b_gate, gla_norm, w_out, s0):
    bsz, t_len, _ = h.shape
    hk, hv = C_HEADS * C_DK, C_HEADS * C_DV
    q, k, v, g_lr, g_out = jnp.split(h @ w_in, [hk, 2 * hk, 2 * hk + hv, 2 * hk + hv + C_GATE_RANK], axis=-1)
    log_a = jax.nn.log_sigmoid((g_lr @ w_gate_lr + b_gate).astype(jnp.float32)) / C_GATE_TAU

    def heads(t, d):
        return t.reshape(bsz, t_len, C_HEADS, d)

    chunk = C_CHUNK if t_len % C_CHUNK == 0 else t_len
    o, s_new = gla_chunked(heads(q, C_DK) * (C_DK ** -0.5), heads(k, C_DK), heads(v, C_DV),
                           heads(log_a, C_DK), s0, chunk)
    o = rms_norm(o, gla_norm) * jax.nn.silu(heads(g_out, C_DV).astype(jnp.float32))
    return o.reshape(bsz, t_len, hv).astype(h.dtype) @ w_out, s_new


def swiglu(h, w_gate, w_up, w_down):
    return (jax.nn.silu(h @ w_gate) * (h @ w_up)) @ w_down


def moe_swiglu(h, w_router, w_gate, w_up, w_down):
    logits = (h @ w_router).astype(jnp.float32)
    top_v, top_i = lax.top_k(logits, TOP_K)
    gates = jax.nn.softmax(top_v, axis=-1)
    dense_gate = jnp.sum(jax.nn.one_hot(top_i, N_EXPERTS, dtype=jnp.float32) * gates[..., None], axis=-2)
    out = jnp.zeros(h.shape, jnp.float32)
    for e in range(N_EXPERTS):
        out = out + dense_gate[..., e:e + 1] * swiglu(h, w_gate[e], w_up[e], w_down[e]).astype(jnp.float32)
    return out.astype(h.dtype)


def setup_inputs(seed: int = 0) -> dict:
    key = jax.random.key(seed)
    ks = iter(jax.random.split(key, 48))
    f32 = jnp.float32

    def nrm(shape, scale=1.0):
        return jax.random.normal(next(ks), shape, f32) * scale

    def gain(shape):
        return 1.0 + nrm(shape, 0.02)

    return {
        'x_prompt': nrm((BATCH, SEQ, D_MODEL)),
        'x_sample': nrm((DEC_BATCH, DEC_SEQ, D_MODEL)),
        'cache_a_w128': nrm((N_EVEN, DEC_BATCH, min(A_WINDOWS[0], PAST_LEN), 2, A_HEADS, HEAD_DIM)),
        'cache_a_w512': nrm((N_EVEN, DEC_BATCH, min(A_WINDOWS[1], PAST_LEN), 2, A_HEADS, HEAD_DIM)),
        'cache_a_w2048': nrm((N_EVEN, DEC_BATCH, min(A_WINDOWS[2], PAST_LEN), 2, A_HEADS, HEAD_DIM)),
        'state_b': nrm((N_EVEN, DEC_BATCH, B_GROUPS, B_STATE, 2), 0.1),
        'state_c': nrm((N_ODD, DEC_BATCH, C_HEADS, C_DK, C_DV), 0.1),
        'norm_mix': gain((DEPTH, D_MODEL)),
        'norm_ffn': gain((DEPTH, D_MODEL)),
        'w_in_even': nrm((N_EVEN, D_MODEL, IN_EVEN), D_MODEL ** -0.5),
        'q_gain': gain((N_EVEN, A_GROUPS, HEAD_DIM)),
        'k_gain': gain((N_EVEN, A_GROUPS, HEAD_DIM)),
        'rel_bias': nrm((REL_BUCKETS, A_GROUPS * A_HEADS), 0.1),
        'lambda_re': -0.5 + nrm((N_EVEN, B_GROUPS, B_STATE), 0.01),
        'lambda_im': jnp.pi * jnp.arange(B_STATE, dtype=f32) + nrm((N_EVEN, B_GROUPS, B_STATE), 0.01),
        'log_dt': jax.random.uniform(next(ks), (N_EVEN, B_GROUPS), f32, math.log(DT_MIN), math.log(DT_MAX)),
        'b_re': nrm((N_EVEN, B_GROUPS, B_STATE, B_GROUP_CH), (2 * B_GROUP_CH) ** -0.5),
        'b_im': nrm((N_EVEN, B_GROUPS, B_STATE, B_GROUP_CH), (2 * B_GROUP_CH) ** -0.5),
        'c_re': nrm((N_EVEN, B_GROUPS, B_GROUP_CH, B_STATE), B_STATE ** -0.5),
        'c_im': nrm((N_EVEN, B_GROUPS, B_GROUP_CH, B_STATE), B_STATE ** -0.5),
        'd_skip': nrm((N_EVEN, B_WIDTH)),
        'w_glu': nrm((N_EVEN, B_WIDTH, B_WIDTH), B_WIDTH ** -0.5),
        'b_glu': nrm((N_EVEN, B_WIDTH), 0.01),
        'w_out_even': nrm((N_EVEN, A_WIDTH + B_WIDTH, D_MODEL), (A_WIDTH + B_WIDTH) ** -0.5),
        'w_ff_gate': nrm((N_EVEN, D_MODEL, D_FF), D_MODEL ** -0.5),
        'w_ff_up': nrm((N_EVEN, D_MODEL, D_FF), D_MODEL ** -0.5),
        'w_ff_down': nrm((N_EVEN, D_FF, D_MODEL), D_FF ** -0.5),
        'w_in_odd': nrm((N_ODD, D_MODEL, IN_ODD), D_MODEL ** -0.5),
        'w_gate_lr': nrm((N_ODD, C_GATE_RANK, C_HEADS * C_DK), C_GATE_RANK ** -0.5),
        'b_gate': nrm((N_ODD, C_HEADS * C_DK), 0.01),
        'gla_norm': gain((N_ODD, C_DV)),
        'w_out_odd': nrm((N_ODD, C_HEADS * C_DV, D_MODEL), (C_HEADS * C_DV) ** -0.5),
        'w_router': nrm((N_ODD, D_MODEL, N_EXPERTS), D_MODEL ** -0.5),
        'w_exp_gate': nrm((N_ODD, N_EXPERTS, D_MODEL, D_EXPERT), D_MODEL ** -0.5),
        'w_exp_up': nrm((N_ODD, N_EXPERTS, D_MODEL, D_EXPERT), D_MODEL ** -0.5),
        'w_exp_down': nrm((N_ODD, N_EXPERTS, D_EXPERT, D_MODEL), D_EXPERT ** -0.5),
    }


def reference(x_prompt, x_sample, cache_a_w128, cache_a_w512, cache_a_w2048, state_b, state_c,
              norm_mix, norm_ffn, w_in_even, q_gain, k_gain, rel_bias, lambda_re, lambda_im, log_dt,
              b_re, b_im, c_re, c_im, d_skip, w_glu, b_glu, w_out_even, w_ff_gate, w_ff_up, w_ff_down,
              w_in_odd, w_gate_lr, b_gate, gla_norm, w_out_odd, w_router, w_exp_gate, w_exp_up, w_exp_down):
    a_caches = (cache_a_w128, cache_a_w512, cache_a_w2048)

    def run(x, sample):
        new_a = [[] for _ in range(A_GROUPS)]
        new_b, new_c = [], []
        for layer in range(DEPTH):
            i = layer // 2
            h = rms_norm(x, norm_mix[layer])
            if layer % 2 == 0:
                bufs = tuple(c[i] for c in a_caches) if sample else None
                mix, kv, sb = even_mixer(h, w_in_even[i], q_gain[i], k_gain[i], rel_bias, lambda_re[i], lambda_im[i],
                                         log_dt[i], b_re[i], b_im[i], c_re[i], c_im[i], d_skip[i], w_glu[i], b_glu[i],
                                         w_out_even[i], bufs, state_b[i] if sample else None)
                x = x + mix
                for g in range(A_GROUPS):
                    new_a[g].append(kv[g])
                new_b.append(sb)
                x = x + swiglu(rms_norm(x, norm_ffn[layer]), w_ff_gate[i], w_ff_up[i], w_ff_down[i])
            else:
                mix, sc = gla_mixer(h, w_in_odd[i], w_gate_lr[i], b_gate[i], gla_norm[i], w_out_odd[i],
                                    state_c[i] if sample else None)
                x = x + mix
                new_c.append(sc)
                x = x + moe_swiglu(rms_norm(x, norm_ffn[layer]), w_router[i], w_exp_gate[i], w_exp_up[i], w_exp_down[i])
        return x, [jnp.stack(a) for a in new_a], jnp.stack(new_b), jnp.stack(new_c)

    y_p, a_p, b_p, c_p = run(x_prompt, False)
    y_s, a_s, b_s, c_s = run(x_sample, True)
    return (y_p, y_s, a_p[0], a_s[0], a_p[1], a_s[1], a_p[2], a_s[2], b_p, b_s, c_p, c_s)
```

```python
import functools
import math

import numpy as np
import jax
import jax.numpy as jnp
from jax import lax
from jax.experimental import pallas as pl
from jax.experimental.pallas import tpu as pltpu

F32 = jnp.float32
BF16 = jnp.bfloat16

D_MODEL = 1024
A_WINDOWS = (128, 512, 2048)
A_DILATIONS = (1, 4, 16)
A_GROUPS = 3
A_HEADS = 8
HEAD_DIM = 64
A_WIDTH = A_HEADS * HEAD_DIM
A_SPAN = 128
A_BLOCK = 128
ATTN_SCALE = HEAD_DIM ** -0.5
REL_BUCKETS = 32
REL_MAX_DIST = 2048
B_WIDTH = 512
B_GROUP_CH = 16
B_GROUPS = 32
B_STATE = 64
C_HEADS = 4
C_DK = 128
C_DV = 256
C_GATE_RANK = 16
C_GATE_TAU = 16.0
C_CHUNK = 64
D_FF = 2752
N_EXPERTS = 8
D_EXPERT = 3584
IN_EVEN = A_GROUPS * 3 * A_WIDTH + B_WIDTH
RMS_EPS = 1e-6
NEG_INF = -1e30

LANES = 128
SUBLANES = 8
VMEM_LIMIT = 48 * 1024 * 1024


def _params(*sem):
    return pltpu.CompilerParams(dimension_semantics=sem, vmem_limit_bytes=VMEM_LIMIT)


def _rel_bucket(dist):
    n = np.maximum(np.asarray(dist), 0)
    exact = REL_BUCKETS // 2
    scaled = np.log(np.maximum(n, 1) / exact) / np.log(REL_MAX_DIST / exact)
    large = np.minimum(exact + (scaled * (REL_BUCKETS - exact)).astype(np.int32), REL_BUCKETS - 1)
    return np.where(n < exact, n, large).astype(np.int32)


def _pick_tile(m, pref):
    for t in pref:
        if m % t == 0:
            return t
    return m


def _norm_matmul_kernel(x_ref, g_ref, w_ref, o_ref, xn_ref):
    @pl.when(pl.program_id(1) == 0)
    def _():
        x = x_ref[...]
        ms = jnp.mean(x * x, axis=-1, keepdims=True)
        xn_ref[...] = (x * lax.rsqrt(ms + RMS_EPS) * g_ref[...]).astype(BF16)

    o_ref[...] = jnp.dot(xn_ref[...], w_ref[...], preferred_element_type=F32)


def norm_matmul(x, g, w, tn):
    m, d = x.shape
    n = w.shape[1]
    tm = _pick_tile(m, (1024, 512, 256, 128))
    return pl.pallas_call(
        _norm_matmul_kernel,
        grid=(m // tm, n // tn),
        in_specs=[
            pl.BlockSpec((tm, d), lambda i, j: (i, 0)),
            pl.BlockSpec((1, d), lambda i, j: (0, 0)),
            pl.BlockSpec((d, tn), lambda i, j: (0, j)),
        ],
        out_specs=pl.BlockSpec((tm, tn), lambda i, j: (i, j)),
        out_shape=jax.ShapeDtypeStruct((m, n), F32),
        scratch_shapes=[pltpu.VMEM((tm, d), BF16)],
        compiler_params=_params("parallel", "arbitrary"),
        name="norm_matmul",
    )(x, g.reshape(1, d), w)


def _head_norm(xh, gain):
    ms = jnp.mean(xh * xh, axis=-1, keepdims=True)
    return xh * lax.rsqrt(ms + RMS_EPS) * gain


def _kv_rows_kernel(k_ref, v_ref, kg_ref, o_ref):
    k = k_ref[0]
    kg = kg_ref[...]
    for h in range(A_HEADS):
        sl = slice(h * HEAD_DIM, (h + 1) * HEAD_DIM)
        o_ref[0, :, sl] = _head_norm(k[:, sl], kg)
    o_ref[0, :, A_WIDTH:] = v_ref[0]


def kv_rows(proj, g, k_gain_g, w):
    bsz, s, _ = proj.shape
    tw = min(A_BLOCK, w)
    first = (s - w) // tw
    return pl.pallas_call(
        _kv_rows_kernel,
        grid=(bsz, w // tw),
        in_specs=[
            pl.BlockSpec((1, tw, A_WIDTH), lambda b, j: (b, first + j, g * 3 + 1)),
            pl.BlockSpec((1, tw, A_WIDTH), lambda b, j: (b, first + j, g * 3 + 2)),
            pl.BlockSpec((1, HEAD_DIM), lambda b, j: (0, 0)),
        ],
        out_specs=pl.BlockSpec((1, tw, 2 * A_WIDTH), lambda b, j: (b, j, 0)),
        out_shape=jax.ShapeDtypeStruct((bsz, w, 2 * A_WIDTH), F32),
        compiler_params=_params("parallel", "parallel"),
        name="kv_rows",
    )(proj, proj, k_gain_g.reshape(1, HEAD_DIM))


def _attn_prompt_kernel(q_ref, kc_ref, kp_ref, vc_ref, vp_ref, tab_ref, qg_ref, kg_ref, o_ref, l_ref):
    q = q_ref[0]
    kc = kc_ref[0]
    kp = kp_ref[0]
    vc = vc_ref[0]
    vp = vp_ref[0]
    qg = qg_ref[...] * ATTN_SCALE
    kg = kg_ref[...]
    for h in range(A_HEADS):
        sl = slice(h * HEAD_DIM, (h + 1) * HEAD_DIM)
        qh = _head_norm(q[:, sl], qg).astype(BF16)
        kh = jnp.concatenate([_head_norm(kp[:, sl], kg), _head_norm(kc[:, sl], kg)], axis=0).astype(BF16)
        vh = jnp.concatenate([vp[:, sl], vc[:, sl]], axis=0).astype(BF16)
        s = lax.dot_general(qh, kh, (((1,), (1,)), ((), ())), preferred_element_type=F32)
        tab = tab_ref[0, h]
        s = jnp.where(tab > 0.5 * NEG_INF, s + tab, NEG_INF)
        m = jnp.max(s, axis=-1, keepdims=True)
        p = jnp.exp(s - m)
        l = jnp.sum(p, axis=-1, keepdims=True)
        oh = jnp.dot(p.astype(BF16), vh, preferred_element_type=F32) / l
        o_ref[0, :, sl] = oh
        l_ref[0, :, sl] = jnp.broadcast_to(m + jnp.log(l), (A_BLOCK, HEAD_DIM))


def attn_prompt_group(proj, g, tab, q_gain_g, k_gain_g):
    bsz, s, n_in = proj.shape
    r = A_DILATIONS[g]
    sub = s // r
    nb = sub // A_BLOCK
    cols = n_in // A_WIDTH
    pv = proj.reshape(bsz, sub, r * n_in)
    base = g * 3

    def qmap(j):
        return lambda b, c, n: (b, n, c * cols + base + j)

    def pmap(j):
        return lambda b, c, n: (b, jnp.maximum(n - 1, 0), c * cols + base + j)

    blk = (1, A_BLOCK, A_WIDTH)
    o, lse = pl.pallas_call(
        _attn_prompt_kernel,
        grid=(bsz, r, nb),
        in_specs=[
            pl.BlockSpec(blk, qmap(0)),
            pl.BlockSpec(blk, qmap(1)),
            pl.BlockSpec(blk, pmap(1)),
            pl.BlockSpec(blk, qmap(2)),
            pl.BlockSpec(blk, pmap(2)),
            pl.BlockSpec((1, A_HEADS, A_BLOCK, 2 * A_BLOCK), lambda b, c, n: (jnp.minimum(n, 1), 0, 0, 0)),
            pl.BlockSpec((1, HEAD_DIM), lambda b, c, n: (0, 0)),
            pl.BlockSpec((1, HEAD_DIM), lambda b, c, n: (0, 0)),
        ],
        out_specs=[
            pl.BlockSpec(blk, lambda b, c, n: (b, n, c)),
            pl.BlockSpec(blk, lambda b, c, n: (b, n, c)),
        ],
        out_shape=[
            jax.ShapeDtypeStruct((bsz, sub, r * A_WIDTH), F32),
            jax.ShapeDtypeStruct((bsz, sub, r * A_WIDTH), F32),
        ],
        compiler_params=_params("parallel", "parallel", "arbitrary"),
        name=f"attn_prompt_g{g}",
    )(pv, pv, pv, pv, pv, tab, q_gain_g.reshape(1, HEAD_DIM), k_gain_g.reshape(1, HEAD_DIM))
    return o.reshape(bsz, s, A_WIDTH), lse.reshape(bsz, s, A_WIDTH)


def _prompt_bias_tables(rel_bias):
    qi = np.arange(A_BLOCK)[:, None]
    kj = np.arange(2 * A_BLOCK)[None, :]
    rel = qi + A_BLOCK - kj
    band = (rel >= 0) & (rel <= A_SPAN)
    first = band & (kj >= A_BLOCK)
    tabs = []
    for g in range(A_GROUPS):
        r = A_DILATIONS[g]
        bias = jnp.transpose(rel_bias[_rel_bucket(r * rel)][:, :, g * A_HEADS:(g + 1) * A_HEADS], (2, 0, 1))
        tabs.append(jnp.stack([jnp.where(first[None], bias, NEG_INF), jnp.where(band[None], bias, NEG_INF)]))
    return tabs


def _attn_sample_kernel(p_ref, c0_ref, c1_ref, c2_ref, tab_ref, qg_ref, kg_ref, o_ref, keys_ref, vals_ref):
    t_new = p_ref.shape[1]
    proj = p_ref[0]
    lane = lax.broadcasted_iota(jnp.int32, (A_HEADS, A_WIDTH), 1)
    row = lax.broadcasted_iota(jnp.int32, (A_HEADS, A_WIDTH), 0)
    head_mask = (lane // HEAD_DIM) == row
    caches = (c0_ref, c1_ref, c2_ref)
    pad = jnp.zeros((A_BLOCK - t_new, A_WIDTH), BF16)
    for t in range(t_new):
        outs, lses = [], []
        for g in range(A_GROUPS):
            base = g * 3 * A_WIDTH
            qg = qg_ref[g:g + 1, :] * ATTN_SCALE
            kg = kg_ref[g:g + 1, :]
            q_row = jnp.concatenate(
                [_head_norm(proj[t:t + 1, base + h * HEAD_DIM: base + (h + 1) * HEAD_DIM], qg)
                 for h in range(A_HEADS)], axis=1)
            k_new = jnp.concatenate(
                [_head_norm(proj[:, base + A_WIDTH + h * HEAD_DIM: base + A_WIDTH + (h + 1) * HEAD_DIM], kg)
                 for h in range(A_HEADS)], axis=1)
            v_new = proj[:, base + 2 * A_WIDTH: base + 3 * A_WIDTH]
            c_ref = caches[g]
            off = 0 if g == 0 else t * 2 * A_WIDTH
            keys_ref[0:A_BLOCK, :] = c_ref[0, :, off:off + A_WIDTH].astype(BF16)
            vals_ref[0:A_BLOCK, :] = c_ref[0, :, off + A_WIDTH:off + 2 * A_WIDTH].astype(BF16)
            keys_ref[A_BLOCK:, :] = jnp.concatenate([k_new.astype(BF16), pad], axis=0)
            vals_ref[A_BLOCK:, :] = jnp.concatenate([v_new.astype(BF16), pad], axis=0)
            q_exp = jnp.where(head_mask, jnp.broadcast_to(q_row, (A_HEADS, A_WIDTH)), 0.0).astype(BF16)
            s = lax.dot_general(q_exp, keys_ref[...], (((1,), (1,)), ((), ())), preferred_element_type=F32)
            tab = tab_ref[g, t]
            s = jnp.where(tab > 0.5 * NEG_INF, s + tab, NEG_INF)
            m = jnp.max(s, axis=-1, keepdims=True)
            p = jnp.exp(s - m)
            l = jnp.sum(p, axis=-1, keepdims=True)
            o = jnp.dot(p.astype(BF16), vals_ref[...], preferred_element_type=F32) / l
            outs.append(o)
            lses.append(m + jnp.log(l))
        mx = jnp.maximum(jnp.maximum(lses[0], lses[1]), lses[2])
        ws = [jnp.exp(x - mx) for x in lses]
        den = ws[0] + ws[1] + ws[2]
        acc = (ws[0] / den) * outs[0] + (ws[1] / den) * outs[1] + (ws[2] / den) * outs[2]
        acc = jnp.where(head_mask, acc, 0.0)
        o_ref[0, t:t + 1, :] = jnp.sum(acc, axis=0, keepdims=True)


def _sample_bias_tables(rel_bias, t_new):
    tabs = np.zeros((A_GROUPS, t_new, 2 * A_BLOCK), np.int32)
    valid = np.zeros((A_GROUPS, t_new, 2 * A_BLOCK), bool)
    for g in range(A_GROUPS):
        r = A_DILATIONS[g]
        for t in range(t_new):
            for i in range(A_BLOCK):
                dist = (A_BLOCK + t - i) if r == 1 else r * (A_BLOCK - i)
                if dist % r == 0 and 0 <= dist // r <= A_SPAN:
                    tabs[g, t, i] = dist
                    valid[g, t, i] = True
            for j in range(t_new):
                dist = t - j
                if dist >= 0 and dist % r == 0 and dist // r <= A_SPAN:
                    tabs[g, t, A_BLOCK + j] = dist
                    valid[g, t, A_BLOCK + j] = True
    out = []
    for g in range(A_GROUPS):
        bias = rel_bias[_rel_bucket(tabs[g])][:, :, g * A_HEADS:(g + 1) * A_HEADS]
        bias = jnp.transpose(bias, (0, 2, 1))
        out.append(jnp.where(valid[g][:, None, :], bias, NEG_INF))
    return jnp.stack(out)


def attn_sample(proj, caches, rel_bias, q_gain, k_gain):
    bsz, t_new, n_in = proj.shape
    tab = _sample_bias_tables(rel_bias, t_new)
    cviews, cspecs = [], []
    for g in range(A_GROUPS):
        r = A_DILATIONS[g]
        w = caches[g].shape[1]
        assert w == A_BLOCK * r and t_new <= min(A_DILATIONS[1:])
        cviews.append(caches[g].reshape(bsz, A_BLOCK, r * 2 * A_WIDTH))
        width = 2 * A_WIDTH if g == 0 else t_new * 2 * A_WIDTH
        cspecs.append(pl.BlockSpec((1, A_BLOCK, width), lambda b: (b, 0, 0)))
    return pl.pallas_call(
        _attn_sample_kernel,
        grid=(bsz,),
        in_specs=[pl.BlockSpec((1, t_new, n_in), lambda b: (b, 0, 0))] + cspecs + [
            pl.BlockSpec(tab.shape, lambda b: (0, 0, 0, 0)),
            pl.BlockSpec((A_GROUPS, HEAD_DIM), lambda b: (0, 0)),
            pl.BlockSpec((A_GROUPS, HEAD_DIM), lambda b: (0, 0)),
        ],
        out_specs=pl.BlockSpec((1, t_new, A_WIDTH), lambda b: (b, 0, 0)),
        out_shape=jax.ShapeDtypeStruct((bsz, t_new, A_WIDTH), F32),
        scratch_shapes=[pltpu.VMEM((2 * A_BLOCK, A_WIDTH), BF16), pltpu.VMEM((2 * A_BLOCK, A_WIDTH), BF16)],
        compiler_params=_params("parallel"),
        name="attn_sample",
    )(proj, *cviews, tab, q_gain, k_gain)


S5_HALF = B_GROUPS * B_STATE // 2
S5_HALF_CH = B_WIDTH // 2
S5_SUB = 256
S5_ROWS = 1024


def _s5_kernel(u_ref, wb_ref, wc_ref, lr_ref, li_ref, h0r_ref, h0i_ref,
               y_ref, hr_ref, hi_ref, bur_ref, bui_ref, *, rs, tc):
    rows = rs * tc
    i = pl.program_id(0)

    @pl.when(i == 0)
    def _():
        hr_ref[...] = h0r_ref[...]
        hi_ref[...] = h0i_ref[...]

    nsub = max(rows // S5_SUB, 1)
    sub = rows // nsub
    odd = (lax.broadcasted_iota(jnp.int32, (sub, 1), 0) % 2) == 1
    for sb in range(nsub):
        rsl = slice(sb * sub, (sb + 1) * sub)
        p = jnp.dot(u_ref[rsl, :], wb_ref[...], preferred_element_type=F32)
        bur_ref[rsl, :] = jnp.where(odd, p[:, 2 * S5_HALF:3 * S5_HALF], p[:, 0:S5_HALF])
        bui_ref[rsl, :] = jnp.where(odd, p[:, 3 * S5_HALF:], p[:, S5_HALF:2 * S5_HALF])

    lr = lr_ref[...]
    li = li_ref[...]

    def step(t, carry):
        hr, hi = carry
        r0 = pl.multiple_of(t * rs, rs)
        nr = lr * hr - li * hi + bur_ref[pl.ds(r0, rs), :]
        ni = lr * hi + li * hr + bui_ref[pl.ds(r0, rs), :]
        bur_ref[pl.ds(r0, rs), :] = nr
        bui_ref[pl.ds(r0, rs), :] = ni
        return nr, ni

    hr, hi = lax.fori_loop(0, tc, step, (hr_ref[...], hi_ref[...]))
    hr_ref[...] = hr
    hi_ref[...] = hi

    for sb in range(nsub):
        rsl = slice(sb * sub, (sb + 1) * sub)
        y = jnp.dot(bur_ref[rsl, :].astype(BF16), wc_ref[0:S5_HALF, :], preferred_element_type=F32)
        y = y + jnp.dot(bui_ref[rsl, :].astype(BF16), wc_ref[S5_HALF:, :], preferred_element_type=F32)
        y_ref[rsl, :] = jnp.where(odd, y[:, S5_HALF_CH:], y[:, 0:S5_HALF_CH])


def _s5_weights(lam_re, lam_im, log_dt, b_re, b_im, c_re, c_im):
    lam = lax.complex(lam_re.astype(F32), lam_im.astype(F32))
    lam_bar = jnp.exp(lam * jnp.exp(log_dt.astype(F32))[:, None])
    b_bar = ((lam_bar - 1.0) / lam)[..., None] * lax.complex(b_re.astype(F32), b_im.astype(F32))
    gh = B_GROUPS // 2
    eye = jnp.eye(gh, dtype=F32)
    bb = jnp.stack([b_bar.real, b_bar.imag], axis=-1).reshape(2, gh, B_STATE, B_GROUP_CH, 2)
    wb = jnp.einsum('hgpcr,xg->xchrgp', bb, eye).reshape(S5_HALF_CH, 4 * S5_HALF)
    cc = jnp.stack([c_re.astype(F32), -c_im.astype(F32)], axis=0).reshape(2, 2, gh, B_GROUP_CH, B_STATE)
    wc = jnp.einsum('rhgcp,gx->rgphxc', cc, eye).reshape(2 * S5_HALF, B_WIDTH)
    lam2 = lam_bar.reshape(2, S5_HALF)
    return wb.astype(BF16), wc.astype(BF16), lam2.real, lam2.imag


def s5_scan(u, h0, weights):
    wb, wc, lam_r, lam_i = weights
    bsz, t_len, _ = u.shape
    rs = 2 * bsz
    assert rs % SUBLANES == 0
    tc = t_len if t_len * rs <= S5_ROWS else S5_ROWS // rs
    assert t_len % tc == 0
    rows = rs * tc
    u2 = jnp.transpose(u.astype(BF16), (1, 0, 2)).reshape(t_len * rs, S5_HALF_CH)
    lr = jnp.tile(lam_r, (bsz, 1))
    li = jnp.tile(lam_i, (bsz, 1))
    if h0 is None:
        h0r = jnp.zeros((rs, S5_HALF), F32)
        h0i = h0r
    else:
        h0r = h0[..., 0].astype(F32).reshape(rs, S5_HALF)
        h0i = h0[..., 1].astype(F32).reshape(rs, S5_HALF)
    const = lambda i: (0, 0)
    y, hr, hi = pl.pallas_call(
        functools.partial(_s5_kernel, rs=rs, tc=tc),
        grid=(t_len // tc,),
        in_specs=[
            pl.BlockSpec((rows, S5_HALF_CH), lambda i: (i, 0)),
            pl.BlockSpec(wb.shape, const),
            pl.BlockSpec(wc.shape, const),
            pl.BlockSpec((rs, S5_HALF), const),
            pl.BlockSpec((rs, S5_HALF), const),
            pl.BlockSpec((rs, S5_HALF), const),
            pl.BlockSpec((rs, S5_HALF), const),
        ],
        out_specs=[
            pl.BlockSpec((rows, S5_HALF_CH), lambda i: (i, 0)),
            pl.BlockSpec((rs, S5_HALF), const),
            pl.BlockSpec((rs, S5_HALF), const),
        ],
        out_shape=[
            jax.ShapeDtypeStruct((t_len * rs, S5_HALF_CH), F32),
            jax.ShapeDtypeStruct((rs, S5_HALF), F32),
            jax.ShapeDtypeStruct((rs, S5_HALF), F32),
        ],
        scratch_shapes=[pltpu.VMEM((rows, S5_HALF), F32), pltpu.VMEM((rows, S5_HALF), F32)],
        compiler_params=_params("arbitrary"),
        name="s5_scan",
    )(u2, wb, wc, lr, li, h0r, h0i)
    y = jnp.transpose(y.reshape(t_len, bsz, B_WIDTH), (1, 0, 2))
    h_last = jnp.stack([hr.reshape(bsz, B_GROUPS, B_STATE), hi.reshape(bsz, B_GROUPS, B_STATE)], axis=-1)
    return y, h_last


def _gelu_tanh(x):
    return 0.5 * x * (1.0 + jnp.tanh(math.sqrt(2.0 / math.pi) * (x + 0.044715 * (x * x * x))))


def _sigmoid(x):
    return 1.0 / (1.0 + jnp.exp(-x))


def _even_out_kernel(*refs, merged):
    if merged:
        oa_ref, y_ref, u_ref, x_ref, d_ref, wg_ref, bg_ref, wo_ref, o_ref = refs
        o_a = oa_ref[...]
    else:
        (o0, o1, o2, l0, l1, l2, y_ref, u_ref, x_ref, d_ref, wg_ref, bg_ref, wo_ref, o_ref) = refs
        a0, a1, a2 = l0[...], l1[...], l2[...]
        mx = jnp.maximum(jnp.maximum(a0, a1), a2)
        e0, e1, e2 = jnp.exp(a0 - mx), jnp.exp(a1 - mx), jnp.exp(a2 - mx)
        den = e0 + e1 + e2
        o_a = (e0 / den) * o0[...] + (e1 / den) * o1[...] + (e2 / den) * o2[...]
    y = _gelu_tanh(y_ref[...] + d_ref[...] * u_ref[...])
    z = jnp.dot(y.astype(BF16), wg_ref[...], preferred_element_type=F32) + bg_ref[...]
    o_b = y * _sigmoid(z)
    acc = jnp.dot(o_a.astype(BF16), wo_ref[0:A_WIDTH, :], preferred_element_type=F32)
    acc = acc + jnp.dot(o_b.astype(BF16), wo_ref[A_WIDTH:, :], preferred_element_type=F32)
    o_ref[...] = x_ref[...] + acc


def even_out(o_parts, y_raw, proj, x, d_skip, w_glu, b_glu, w_out):
    m = x.shape[0]
    tm = _pick_tile(m, (512, 256, 128))
    ucol = (A_GROUPS * 3 * A_WIDTH) // B_WIDTH
    row = lambda i: (i, 0)
    const = lambda i: (0, 0)
    in_specs = [pl.BlockSpec((tm, A_WIDTH), row) for _ in o_parts] + [
        pl.BlockSpec((tm, B_WIDTH), row),
        pl.BlockSpec((tm, B_WIDTH), lambda i: (i, ucol)),
        pl.BlockSpec((tm, D_MODEL), row),
        pl.BlockSpec((1, B_WIDTH), const),
        pl.BlockSpec((B_WIDTH, B_WIDTH), const),
        pl.BlockSpec((1, B_WIDTH), const),
        pl.BlockSpec((A_WIDTH + B_WIDTH, D_MODEL), const),
    ]
    return pl.pallas_call(
        functools.partial(_even_out_kernel, merged=len(o_parts) == 1),
        grid=(m // tm,),
        in_specs=in_specs,
        out_specs=pl.BlockSpec((tm, D_MODEL), row),
        out_shape=jax.ShapeDtypeStruct((m, D_MODEL), F32),
        compiler_params=_params("parallel"),
        name="even_out",
    )(*o_parts, y_raw, proj, x, d_skip.reshape(1, B_WIDTH), w_glu, b_glu.reshape(1, B_WIDTH), w_out)


def _silu(x):
    return x * _sigmoid(x)


def _ffn_kernel(x_ref, g_ref, wg_ref, wu_ref, wd_ref, o_ref, xn_ref, acc_ref):
    f = pl.program_id(1)

    @pl.when(f == 0)
    def _():
        x = x_ref[...]
        ms = jnp.mean(x * x, axis=-1, keepdims=True)
        xn_ref[...] = (x * lax.rsqrt(ms + RMS_EPS) * g_ref[...]).astype(BF16)
        acc_ref[...] = jnp.zeros_like(acc_ref)

    xn = xn_ref[...]
    hg = jnp.dot(xn, wg_ref[...], preferred_element_type=F32)
    hu = jnp.dot(xn, wu_ref[...], preferred_element_type=F32)
    act = (_silu(hg) * hu).astype(BF16)
    acc_ref[...] += jnp.dot(act, wd_ref[...], preferred_element_type=F32)

    @pl.when(f == pl.num_programs(1) - 1)
    def _():
        o_ref[...] = x_ref[...] + acc_ref[...]


def ffn(x, g, w_gate, w_up, w_down, tf):
    m, d = x.shape
    fdim = w_gate.shape[1]
    tm = _pick_tile(m, (1024, 512, 256, 128))
    return pl.pallas_call(
        _ffn_kernel,
        grid=(m // tm, fdim // tf),
        in_specs=[
            pl.BlockSpec((tm, d), lambda i, f: (i, 0)),
            pl.BlockSpec((1, d), lambda i, f: (0, 0)),
            pl.BlockSpec((d, tf), lambda i, f: (0, f)),
            pl.BlockSpec((d, tf), lambda i, f: (0, f)),
            pl.BlockSpec((tf, d), lambda i, f: (f, 0)),
        ],
        out_specs=pl.BlockSpec((tm, d), lambda i, f: (i, 0)),
        out_shape=jax.ShapeDtypeStruct((m, d), F32),
        scratch_shapes=[pltpu.VMEM((tm, d), BF16), pltpu.VMEM((tm, d), F32)],
        compiler_params=_params("parallel", "arbitrary"),
        name="ffn",
    )(x, g.reshape(1, d), w_gate, w_up, w_down)


GLA_Q = 0
GLA_K = C_HEADS * C_DK
GLA_V = 2 * C_HEADS * C_DK
GLA_GO = GLA_V + C_HEADS * C_DV
GLA_LR = GLA_GO + C_HEADS * C_DV
GLA_IN = GLA_LR + LANES


def _log_sigmoid(x):
    return jnp.minimum(x, 0.0) - jnp.log(1.0 + jnp.exp(-jnp.abs(x)))


def _split3(x):
    a = x.astype(BF16)
    r = x - a.astype(F32)
    b = r.astype(BF16)
    c = (r - b.astype(F32)).astype(BF16)
    return a, b, c


def _gla_kernel(p_ref, wlr_ref, bg_ref, gn_ref, s0_ref, o_ref, sf_ref, s_ref, b_ref, *, chunk, n_chunks, valid):
    step = pl.program_id(1)

    @pl.when(step == 0)
    def _():
        s_ref[...] = s0_ref[0]

    rows = chunk * n_chunks
    z = jnp.dot(p_ref[0, :, GLA_LR:].astype(BF16), wlr_ref[...], preferred_element_type=F32) + bg_ref[...]
    la = _log_sigmoid(z) * (1.0 / C_GATE_TAU)
    if valid < chunk:
        la = jnp.where((lax.broadcasted_iota(jnp.int32, (rows, 1), 0) % chunk) < valid, la, 0.0)
    ri = lax.broadcasted_iota(jnp.int32, (rows, rows), 0)
    ci = lax.broadcasted_iota(jnp.int32, (rows, rows), 1)
    tri = jnp.where(((ri // chunk) == (ci // chunk)) & (ci <= ri), 1.0, 0.0).astype(BF16)
    a1, a2, a3 = _split3(la)
    b_ref[...] = (jnp.dot(tri, a1, preferred_element_type=F32) + jnp.dot(tri, a2, preferred_element_type=F32)
                  + jnp.dot(tri, a3, preferred_element_type=F32))
    ri_c = lax.broadcasted_iota(jnp.int32, (chunk, chunk), 0)
    ci_c = lax.broadcasted_iota(jnp.int32, (chunk, chunk), 1)
    causal = ci_c <= ri_c
    gn = gn_ref[...]
    for c in range(n_chunks):
        rsl = slice(c * chunk, (c + 1) * chunk)
        for h in range(C_HEADS):
            ksl = slice(h * C_DK, (h + 1) * C_DK)
            vsl = slice(h * C_DV, (h + 1) * C_DV)
            q = p_ref[0, rsl, GLA_Q + h * C_DK: GLA_Q + (h + 1) * C_DK] * (C_DK ** -0.5)
            k = p_ref[0, rsl, GLA_K + h * C_DK: GLA_K + (h + 1) * C_DK]
            v = p_ref[0, rsl, GLA_V + h * C_DV: GLA_V + (h + 1) * C_DV].astype(BF16)
            b = b_ref[rsl, ksl]
            b_last = b_ref[(c + 1) * chunk - 1:(c + 1) * chunk, ksl]
            q_dec = (q * jnp.exp(b)).astype(BF16)
            k_inv = (k * jnp.exp(-b)).astype(BF16)
            k_tail = k * jnp.exp(b_last - b)
            att = lax.dot_general(q_dec, k_inv, (((1,), (1,)), ((), ())), preferred_element_type=F32)
            att = jnp.where(causal, att, 0.0).astype(BF16)
            s_old = s_ref[h]
            o = jnp.dot(att, v, preferred_element_type=F32)
            o = o + jnp.dot(q_dec, s_old.astype(BF16), preferred_element_type=F32)
            dec_col = jnp.exp(jnp.transpose(jnp.broadcast_to(b_last, (SUBLANES, C_DK))))[:, 0:1]
            kv = jnp.dot(jnp.transpose(k_tail).astype(BF16), v, preferred_element_type=F32)
            s_ref[h] = s_old * dec_col + kv
            ms = jnp.mean(o * o, axis=-1, keepdims=True)
            on = o * lax.rsqrt(ms + RMS_EPS) * gn
            o_ref[0, rsl, vsl] = on * _silu(p_ref[0, rsl, GLA_GO + h * C_DV: GLA_GO + (h + 1) * C_DV])

    @pl.when(step == pl.num_programs(1) - 1)
    def _():
        sf_ref[0] = s_ref[...]


def gla(proj, w_lr, b_gate, gla_norm, s0, chunk, n_chunks, valid):
    bsz, t_len, _ = proj.shape
    rows = chunk * n_chunks
    hv = C_HEADS * C_DV
    if s0 is None:
        s0 = jnp.zeros((bsz, C_HEADS, C_DK, C_DV), F32)
    return pl.pallas_call(
        functools.partial(_gla_kernel, chunk=chunk, n_chunks=n_chunks, valid=valid),
        grid=(bsz, t_len // rows),
        in_specs=[
            pl.BlockSpec((1, rows, GLA_IN), lambda b, i: (b, i, 0)),
            pl.BlockSpec((LANES, C_HEADS * C_DK), lambda b, i: (0, 0)),
            pl.BlockSpec((1, C_HEADS * C_DK), lambda b, i: (0, 0)),
            pl.BlockSpec((1, C_DV), lambda b, i: (0, 0)),
            pl.BlockSpec((1, C_HEADS, C_DK, C_DV), lambda b, i: (b, 0, 0, 0)),
        ],
        out_specs=[
            pl.BlockSpec((1, rows, hv), lambda b, i: (b, i, 0)),
            pl.BlockSpec((1, C_HEADS, C_DK, C_DV), lambda b, i: (b, 0, 0, 0)),
        ],
        out_shape=[
            jax.ShapeDtypeStruct((bsz, t_len, hv), F32),
            jax.ShapeDtypeStruct((bsz, C_HEADS, C_DK, C_DV), F32),
        ],
        scratch_shapes=[pltpu.VMEM((C_HEADS, C_DK, C_DV), F32), pltpu.VMEM((rows, C_HEADS * C_DK), F32)],
        compiler_params=_params("parallel", "arbitrary"),
        name="gla",
    )(proj, w_lr, b_gate.reshape(1, -1), gla_norm.reshape(1, C_DV), s0.astype(F32))


def _matmul_res_kernel(a_ref, w_ref, x_ref, o_ref):
    o_ref[...] = x_ref[...] + jnp.dot(a_ref[...].astype(BF16), w_ref[...], preferred_element_type=F32)


def matmul_res(a, w, x):
    m, k = a.shape
    n = w.shape[1]
    tm = _pick_tile(m, (512, 256, 128))
    return pl.pallas_call(
        _matmul_res_kernel,
        grid=(m // tm,),
        in_specs=[
            pl.BlockSpec((tm, k), lambda i: (i, 0)),
            pl.BlockSpec((k, n), lambda i: (0, 0)),
            pl.BlockSpec((tm, n), lambda i: (i, 0)),
        ],
        out_specs=pl.BlockSpec((tm, n), lambda i: (i, 0)),
        out_shape=jax.ShapeDtypeStruct((m, n), F32),
        compiler_params=_params("parallel"),
        name="matmul_res",
    )(a, w, x)


def _moe_kernel(x_ref, g_ref, wr_ref, wg_ref, wu_ref, wd_ref, o_ref, xn_ref, gate_ref, acc_ref):
    e = pl.program_id(1)
    f = pl.program_id(2)

    @pl.when((e == 0) & (f == 0))
    def _():
        x = x_ref[...]
        ms = jnp.mean(x * x, axis=-1, keepdims=True)
        xn = x * lax.rsqrt(ms + RMS_EPS) * g_ref[...]
        xn_ref[...] = xn.astype(BF16)
        acc_ref[...] = jnp.zeros_like(acc_ref)
        x1, x2, x3 = _split3(xn)
        w1, w2, w3 = wr_ref[0], wr_ref[1], wr_ref[2]
        logits = (jnp.dot(x1, w1, preferred_element_type=F32) + jnp.dot(x1, w2, preferred_element_type=F32)
                  + jnp.dot(x2, w1, preferred_element_type=F32) + jnp.dot(x1, w3, preferred_element_type=F32)
                  + jnp.dot(x2, w2, preferred_element_type=F32) + jnp.dot(x3, w1, preferred_element_type=F32))
        lane = lax.broadcasted_iota(jnp.int32, logits.shape, 1)
        logits = jnp.where(lane < N_EXPERTS, logits, -3e38)
        m1 = jnp.max(logits, axis=-1, keepdims=True)
        i1 = jnp.min(jnp.where(logits == m1, lane, LANES), axis=-1, keepdims=True)
        rest = jnp.where(lane == i1, -3e38, logits)
        m2 = jnp.max(rest, axis=-1, keepdims=True)
        i2 = jnp.min(jnp.where(rest == m2, lane, LANES), axis=-1, keepdims=True)
        e2 = jnp.exp(m2 - m1)
        den = 1.0 + e2
        gate_ref[...] = jnp.where(lane == i1, 1.0 / den, 0.0) + jnp.where(lane == i2, e2 / den, 0.0)

    xn = xn_ref[...]
    hg = jnp.dot(xn, wg_ref[0], preferred_element_type=F32)
    hu = jnp.dot(xn, wu_ref[0], preferred_element_type=F32)
    act = (_silu(hg) * hu).astype(BF16)
    part = jnp.dot(act, wd_ref[0], preferred_element_type=F32)
    gates = gate_ref[...]
    lane = lax.broadcasted_iota(jnp.int32, gates.shape, 1)
    gcol = jnp.sum(jnp.where(lane == e, gates, 0.0), axis=-1, keepdims=True)
    acc_ref[...] += gcol * part

    @pl.when((e == pl.num_programs(1) - 1) & (f == pl.num_programs(2) - 1))
    def _():
        o_ref[...] = x_ref[...] + acc_ref[...]


def moe(x, g, w_router3, w_gate, w_up, w_down, tf):
    m, d = x.shape
    n_e, _, fdim = w_gate.shape
    tm = _pick_tile(m, (1024, 512, 256, 128))
    return pl.pallas_call(
        _moe_kernel,
        grid=(m // tm, n_e, fdim // tf),
        in_specs=[
            pl.BlockSpec((tm, d), lambda i, e, f: (i, 0)),
            pl.BlockSpec((1, d), lambda i, e, f: (0, 0)),
            pl.BlockSpec((3, d, LANES), lambda i, e, f: (0, 0, 0)),
            pl.BlockSpec((1, d, tf), lambda i, e, f: (e, 0, f)),
            pl.BlockSpec((1, d, tf), lambda i, e, f: (e, 0, f)),
            pl.BlockSpec((1, tf, d), lambda i, e, f: (e, f, 0)),
        ],
        out_specs=pl.BlockSpec((tm, d), lambda i, e, f: (i, 0)),
        out_shape=jax.ShapeDtypeStruct((m, d), F32),
        scratch_shapes=[pltpu.VMEM((tm, d), BF16), pltpu.VMEM((tm, LANES), F32), pltpu.VMEM((tm, d), F32)],
        compiler_params=_params("parallel", "arbitrary", "arbitrary"),
        name="moe",
    )(x, g.reshape(1, d), w_router3, w_gate, w_up, w_down)


def _pad_cols(w, mult):
    n = w.shape[-1]
    pad = (-n) % mult
    if pad == 0:
        return w
    return jnp.pad(w, [(0, 0)] * (w.ndim - 1) + [(0, pad)])


def _pad_rows(w, mult):
    n = w.shape[-2]
    pad = (-n) % mult
    if pad == 0:
        return w
    return jnp.pad(w, [(0, 0)] * (w.ndim - 2) + [(0, pad), (0, 0)])


def _even_layer(x, sample, i, P, a_bufs, b_state):
    bsz, t_len, _ = x.shape
    m = bsz * t_len
    xf = x.reshape(m, D_MODEL)
    proj = norm_matmul(xf, P['norm_mix'][2 * i], P['w_in_even'][i], tn=IN_EVEN // 5)
    proj3 = proj.reshape(bsz, t_len, IN_EVEN)
    new_kv = []
    if not sample:
        tabs = _prompt_bias_tables(P['rel_bias'])
        parts_o, parts_l = [], []
        for g in range(A_GROUPS):
            o, lse = attn_prompt_group(proj3, g, tabs[g], P['q_gain'][i, g], P['k_gain'][i, g])
            parts_o.append(o.reshape(m, A_WIDTH))
            parts_l.append(lse.reshape(m, A_WIDTH))
            w = min(A_WINDOWS[g], t_len)
            kv = kv_rows(proj3, g, P['k_gain'][i, g], w)
            new_kv.append(kv.reshape(bsz, w, 2, A_HEADS, HEAD_DIM))
        o_parts = tuple(parts_o + parts_l)
    else:
        o_a = attn_sample(proj3, a_bufs, P['rel_bias'], P['q_gain'][i], P['k_gain'][i])
        o_parts = (o_a.reshape(m, A_WIDTH),)
        flat = proj.reshape(1, m, IN_EVEN)
        for g in range(A_GROUPS):
            kv = kv_rows(flat, g, P['k_gain'][i, g], m)
            new_kv.append(kv.reshape(bsz, t_len, 2, A_HEADS, HEAD_DIM))
    u = proj3[..., A_GROUPS * 3 * A_WIDTH:]
    y_raw, new_b = s5_scan(u, b_state, P['s5'][i])
    x1 = even_out(o_parts, y_raw.reshape(m, B_WIDTH), proj, xf, P['d_skip'][i], P['w_glu'][i], P['b_glu'][i],
                  P['w_out_even'][i])
    x2 = ffn(x1, P['norm_ffn'][2 * i], P['w_ff_gate'][i], P['w_ff_up'][i], P['w_ff_down'][i], tf=256)
    return x2.reshape(bsz, t_len, D_MODEL), new_kv, new_b


def _odd_layer(x, sample, i, P, s0):
    bsz, t_len, _ = x.shape
    m = bsz * t_len
    xf = x.reshape(m, D_MODEL)
    proj = norm_matmul(xf, P['norm_mix'][2 * i + 1], P['w_in_odd'][i], tn=GLA_IN // 5)
    proj3 = proj.reshape(bsz, t_len, GLA_IN)
    if t_len % C_CHUNK == 0:
        o, s_new = gla(proj3, P['w_gate_lr'][i], P['b_gate'][i], P['gla_norm'][i], s0, C_CHUNK, 4, C_CHUNK)
    else:
        t_pad = -(-t_len // SUBLANES) * SUBLANES
        pp = jnp.pad(proj3, ((0, 0), (0, t_pad - t_len), (0, 0)))
        o, s_new = gla(pp, P['w_gate_lr'][i], P['b_gate'][i], P['gla_norm'][i], s0, t_pad, 1, t_len)
        o = o[:, :t_len]
    x1 = matmul_res(o.reshape(m, C_HEADS * C_DV), P['w_out_odd'][i], xf)
    x2 = moe(x1, P['norm_ffn'][2 * i + 1], P['w_router'][i], P['w_exp_gate'][i], P['w_exp_up'][i],
             P['w_exp_down'][i], tf=256)
    return x2.reshape(bsz, t_len, D_MODEL), s_new


def kernel(x_prompt, x_sample, cache_a_w128, cache_a_w512, cache_a_w2048, state_b, state_c, norm_mix, norm_ffn, w_in_even, q_gain, k_gain, rel_bias, lambda_re, lambda_im, log_dt, b_re, b_im, c_re, c_im, d_skip, w_glu, b_glu, w_out_even, w_ff_gate, w_ff_up, w_ff_down, w_in_odd, w_gate_lr, b_gate, gla_norm, w_out_odd, w_router, w_exp_gate, w_exp_up, w_exp_down):
    n_even = w_in_even.shape[0]
    n_odd = w_in_odd.shape[0]
    hk, hv = C_HEADS * C_DK, C_HEADS * C_DV
    w_odd = jnp.concatenate([w_in_odd[..., :2 * hk + hv], w_in_odd[..., 2 * hk + hv + C_GATE_RANK:],
                             _pad_cols(w_in_odd[..., 2 * hk + hv:2 * hk + hv + C_GATE_RANK], LANES)], axis=-1)
    wr = _pad_cols(w_router.astype(F32), LANES)
    wr1 = wr.astype(BF16)
    wr2 = (wr - wr1.astype(F32)).astype(BF16)
    wr3 = (wr - wr1.astype(F32) - wr2.astype(F32)).astype(BF16)
    P = {
        'norm_mix': norm_mix, 'norm_ffn': norm_ffn,
        'w_in_even': w_in_even.astype(BF16), 'q_gain': q_gain, 'k_gain': k_gain, 'rel_bias': rel_bias,
        's5': [_s5_weights(lambda_re[i], lambda_im[i], log_dt[i], b_re[i], b_im[i], c_re[i], c_im[i])
               for i in range(n_even)],
        'd_skip': d_skip, 'w_glu': w_glu.astype(BF16), 'b_glu': b_glu, 'w_out_even': w_out_even.astype(BF16),
        'w_ff_gate': _pad_cols(w_ff_gate, 256).astype(BF16), 'w_ff_up': _pad_cols(w_ff_up, 256).astype(BF16),
        'w_ff_down': _pad_rows(w_ff_down, 256).astype(BF16),
        'w_in_odd': w_odd.astype(BF16),
        'w_gate_lr': _pad_rows(w_gate_lr, LANES).astype(BF16), 'b_gate': b_gate, 'gla_norm': gla_norm,
        'w_out_odd': w_out_odd.astype(BF16),
        'w_router': jnp.stack([wr1, wr2, wr3], axis=1),
        'w_exp_gate': w_exp_gate.astype(BF16), 'w_exp_up': w_exp_up.astype(BF16),
        'w_exp_down': w_exp_down.astype(BF16),
    }
    a_caches = (cache_a_w128, cache_a_w512, cache_a_w2048)

    def run(x, sample):
        new_a = [[] for _ in range(A_GROUPS)]
        new_b, new_c = [], []
        for layer in range(n_even + n_odd):
            i = layer // 2
            if layer % 2 == 0:
                bufs = tuple(c[i] for c in a_caches) if sample else None
                x, kv, sb = _even_layer(x, sample, i, P, bufs, state_b[i] if sample else None)
                for g in range(A_GROUPS):
                    new_a[g].append(kv[g])
                new_b.append(sb)
            else:
                x, sc = _odd_layer(x, sample, i, P, state_c[i] if sample else None)
                new_c.append(sc)
        return x, [jnp.stack(a) for a in new_a], jnp.stack(new_b), jnp.stack(new_c)

    y_p, a_p, b_p, c_p = run(x_prompt, False)
    y_s, a_s, b_s, c_s = run(x_sample, True)
    return (y_p, y_s, a_p[0], a_s[0], a_p[1], a_s[1], a_p[2], a_s[2], b_p, b_s, c_p, c_s)
```
